```python
import math
import jax, jax.numpy as jnp
from jax import lax
import numpy as np

D_MODEL = 2048
BATCH = 8
SEQ = 2048
DEPTH = 1
DEC_BATCH = 128
DEC_SEQ = 8
PAST_LEN = 2048
PAGE_SIZE = 128

HEAD_DIM = 128
H_A = 8
H_B = 8
H_M = 4
N_MEM = 256
MOBA_BLOCK = 256
MOBA_TOPK = 3
DSA_TOPK = 256
IDX_HEADS = 16
IDX_DIM = 64
Q_BLOCK = 128
W_A = H_A * HEAD_DIM
W_B = H_B * HEAD_DIM
W_M = H_M * HEAD_DIM
N_BRANCH = 3
IN_SPLITS = (W_A, W_A, W_A, W_A, W_B, W_B, W_B, W_B, IDX_HEADS * IDX_DIM, IDX_DIM, IDX_HEADS, W_M, W_M, N_BRANCH * D_MODEL)
D_IN = sum(IN_SPLITS)
DEEPNORM_ALPHA = (2.0 * DEPTH) ** 0.25
DEEPNORM_BETA = (8.0 * DEPTH) ** -0.25
LN_EPS = 1e-5

kernel_name = "hybrid_moba_dsa_memory_gated_decoder_step"


def _alibi_slopes():
    n = H_A + H_B
    return 2.0 ** (-8.0 * jnp.arange(1, n + 1, dtype=jnp.float32) / n)


def _layernorm(x, g, b):
    xf = x.astype(jnp.float32)
    mu = xf.mean(-1, keepdims=True)
    var = jnp.square(xf - mu).mean(-1, keepdims=True)
    return ((xf - mu) * lax.rsqrt(var + LN_EPS) * g.astype(jnp.float32) + b.astype(jnp.float32)).astype(x.dtype)


def _map_query_blocks(fn, xs):
    q_len = xs[0].shape[0]
    c = math.gcd(q_len, Q_BLOCK)
    xs_b = tuple(a.reshape((q_len // c, c) + a.shape[1:]) for a in xs)
    out = lax.map(fn, xs_b)
    return out.reshape((q_len,) + out.shape[2:])


def _moba_attend(q, k, v, q_pos, slopes):
    L = k.shape[0]
    n_blk = -(-L // MOBA_BLOCK)
    pad = n_blk * MOBA_BLOCK - L

    def to_blocks(a):
        a = jnp.pad(a, ((0, pad), (0, 0), (0, 0)))
        return a.reshape(n_blk, MOBA_BLOCK, H_A, HEAD_DIM).transpose(2, 0, 1, 3)

    kb, vb = to_blocks(k), to_blocks(v)
    k_mean = kb.astype(jnp.float32).mean(axis=2)
    kk = min(MOBA_TOPK, n_blk)
    scale = HEAD_DIM ** -0.5
    h_idx = jnp.arange(H_A)[None, :, None]
    offs = jnp.arange(MOBA_BLOCK, dtype=jnp.int32)
    blk_ids = jnp.arange(n_blk, dtype=jnp.int32)

    def block_fn(args):
        qc, tc = args
        c = qc.shape[0]
        qf = qc.astype(jnp.float32)
        bt = tc // MOBA_BLOCK
        gate = jnp.einsum('chd,hnd->chn', qf, k_mean)
        gate = jnp.where(blk_ids[None, None, :] < bt[:, None, None], gate, -jnp.inf)
        g_top, i_top = lax.top_k(gate, kk)
        blk = jnp.concatenate([i_top.astype(jnp.int32), jnp.broadcast_to(bt[:, None, None], (c, H_A, 1))], axis=-1)
        keep = jnp.concatenate([jnp.isfinite(g_top), jnp.ones((c, H_A, 1), dtype=bool)], axis=-1)
        kg = kb[h_idx, blk].astype(jnp.float32)
        vg = vb[h_idx, blk].astype(jnp.float32)
        s_pos = blk[..., None] * MOBA_BLOCK + offs
        dist = (tc[:, None, None, None] - s_pos).astype(jnp.float32)
        logits = jnp.einsum('chd,chnkd->chnk', qf, kg) * scale - slopes[None, :, None, None] * dist
        mask = keep[..., None] & (dist >= 0)
        logits = jnp.where(mask, logits, -jnp.inf).reshape(c, H_A, (kk + 1) * MOBA_BLOCK)
        p = jax.nn.softmax(logits, axis=-1).reshape(c, H_A, kk + 1, MOBA_BLOCK)
        return jnp.einsum('chnk,chnkd->chd', p, vg).astype(q.dtype)

    return _map_query_blocks(block_fn, (q, q_pos))


def _dsa_attend(q, k, v, qi, wi, ki, q_pos, slopes):
    L = k.shape[0]
    n_sel = min(DSA_TOPK, L // 4)
    scale = HEAD_DIM ** -0.5
    idx_scale = IDX_DIM ** -0.5
    kif = ki.astype(jnp.float32)
    key_pos = jnp.arange(L, dtype=jnp.int32)

    def block_fn(args):
        qc, qic, wic, tc = args
        rel = jax.nn.relu(jnp.einsum('cjd,sd->cjs', qic.astype(jnp.float32), kif) * idx_scale)
        score = jnp.einsum('cj,cjs->cs', wic.astype(jnp.float32), rel)
        score = jnp.where(key_pos[None, :] <= tc[:, None], score, -jnp.inf)
        s_top, i_top = lax.top_k(score, n_sel)
        keep = jnp.isfinite(s_top)
        kg = k[i_top].astype(jnp.float32)
        vg = v[i_top].astype(jnp.float32)
        dist = (tc[:, None] - i_top).astype(jnp.float32)
        logits = jnp.einsum('chd,cnhd->chn', qc.astype(jnp.float32), kg) * scale - slopes[None, :, None] * dist[:, None, :]
        logits = jnp.where(keep[:, None, :], logits, -jnp.inf)
        p = jax.nn.softmax(logits, axis=-1)
        return jnp.einsum('chn,cnhd->chd', p, vg).astype(q.dtype)

    return _map_query_blocks(block_fn, (q, qi, wi, q_pos))


def _mem_attend(q, mk, mv):
    logits = jnp.einsum('nshd,nmhd->nhsm', q.astype(jnp.float32), mk.astype(jnp.float32)) * HEAD_DIM ** -0.5
    p = jax.nn.softmax(logits, axis=-1)
    return jnp.einsum('nhsm,nmhd->nshd', p, mv.astype(jnp.float32)).astype(q.dtype)


def _in_proj(x, w_in, b_in):
    n, t = x.shape[:2]
    h = jnp.einsum('nsd,de->nse', x, w_in) + b_in
    cuts = np.cumsum(IN_SPLITS)[:-1].tolist()
    qa, ka, va, za, qb, kb, vb, zb, qi, ki, wi, qm, zm, gates = jnp.split(h, cuts, axis=-1)
    hd = lambda a, nh, d: a.reshape(n, t, nh, d)
    return (hd(qa, H_A, HEAD_DIM), hd(ka, H_A, HEAD_DIM), hd(va, H_A, HEAD_DIM), za,
            hd(qb, H_B, HEAD_DIM), hd(kb, H_B, HEAD_DIM), hd(vb, H_B, HEAD_DIM), zb,
            hd(qi, IDX_HEADS, IDX_DIM), ki, wi * IDX_HEADS ** -0.5,
            hd(qm, H_M, HEAD_DIM), zm, gates)


def _mem_kv(mem, w_mem_kv):
    n = mem.shape[0]
    kv = jnp.einsum('nmd,de->nme', mem, w_mem_kv)
    mk, mv = jnp.split(kv, 2, axis=-1)
    return mk.reshape(n, N_MEM, H_M, HEAD_DIM), mv.reshape(n, N_MEM, H_M, HEAD_DIM)


def _merge(x, o_a, z_a, o_b, z_b, o_m, z_m, gates, w_br_a, w_br_b, w_br_m, w_out, ln_g, ln_b):
    n, t = x.shape[:2]
    u_a = jnp.einsum('nse,ed->nsd', o_a.reshape(n, t, W_A) * jax.nn.silu(z_a), w_br_a)
    u_b = jnp.einsum('nse,ed->nsd', o_b.reshape(n, t, W_B) * jax.nn.silu(z_b), w_br_b)
    u_m = jnp.einsum('nse,ed->nsd', o_m.reshape(n, t, W_M) * jax.nn.silu(z_m), w_br_m)
    g = jax.nn.sigmoid(gates.reshape(n, t, N_BRANCH, D_MODEL))
    mixed = g[:, :, 0] * u_a + g[:, :, 1] * u_b + g[:, :, 2] * u_m
    y = jnp.einsum('nsd,de->nse', mixed, w_out)
    return _layernorm(DEEPNORM_ALPHA * x + y, ln_g, ln_b)


def setup_inputs(seed: int = 0) -> dict:
    key = jax.random.key(seed)
    ks = jax.random.split(key, 24)
    f32 = jnp.float32
    n_pages = PAST_LEN // PAGE_SIZE
    n_used = DEC_BATCH * n_pages
    n_phys = n_used + max(1, n_used // 4)
    nrm = lambda k, shape, s=1.0: s * jax.random.normal(k, shape, f32)
    page_table = jax.random.permutation(ks[0], n_phys)[:n_used].reshape(DEC_BATCH, n_pages).astype(jnp.int32)
    beta = DEEPNORM_BETA
    return {
        "x_prompt": nrm(ks[1], (BATCH, SEQ, D_MODEL)),
        "x_sample": nrm(ks[2], (DEC_BATCH, DEC_SEQ, D_MODEL)),
        "cache_a_k": nrm(ks[3], (DEPTH, n_phys, PAGE_SIZE, H_A, HEAD_DIM)),
        "cache_a_v": nrm(ks[4], (DEPTH, n_phys, PAGE_SIZE, H_A, HEAD_DIM)),
        "cache_b_k": nrm(ks[5], (DEPTH, n_phys, PAGE_SIZE, H_B, HEAD_DIM)),
        "cache_b_v": nrm(ks[6], (DEPTH, n_phys, PAGE_SIZE, H_B, HEAD_DIM)),
        "cache_b_idx_k": nrm(ks[7], (DEPTH, n_phys, PAGE_SIZE, IDX_DIM)),
        "cache_mem_k": nrm(ks[8], (DEPTH, DEC_BATCH, N_MEM, H_M, HEAD_DIM)),
        "cache_mem_v": nrm(ks[9], (DEPTH, DEC_BATCH, N_MEM, H_M, HEAD_DIM)),
        "page_table": page_table,
        "mem_prompt": nrm(ks[10], (BATCH, N_MEM, D_MODEL)),
        "w_in": nrm(ks[11], (DEPTH, D_MODEL, D_IN), D_MODEL ** -0.5),
        "b_in": nrm(ks[12], (DEPTH, D_IN), 0.01),
        "w_mem_kv": nrm(ks[13], (DEPTH, D_MODEL, 2 * W_M), D_MODEL ** -0.5),
        "w_br_a": nrm(ks[14], (DEPTH, W_A, D_MODEL), beta * W_A ** -0.5),
        "w_br_b": nrm(ks[15], (DEPTH, W_B, D_MODEL), beta * W_B ** -0.5),
        "w_br_m": nrm(ks[16], (DEPTH, W_M, D_MODEL), beta * W_M ** -0.5),
        "w_out": nrm(ks[17], (DEPTH, D_MODEL, D_MODEL), beta * D_MODEL ** -0.5),
        "ln_g": 1.0 + nrm(ks[18], (DEPTH, D_MODEL), 0.01),
        "ln_b": nrm(ks[19], (DEPTH, D_MODEL), 0.01),
    }


def reference(x_prompt, x_sample, cache_a_k, cache_a_v, cache_b_k, cache_b_v, cache_b_idx_k,
              cache_mem_k, cache_mem_v, page_table, mem_prompt,
              w_in, b_in, w_mem_kv, w_br_a, w_br_b, w_br_m, w_out, ln_g, ln_b):
    slopes = _alibi_slopes()
    sl_a, sl_b = slopes[:H_A], slopes[H_A:]
    xp, xs = x_prompt, x_sample
    pos_p = jnp.arange(xp.shape[1], dtype=jnp.int32)
    pos_s = PAST_LEN + jnp.arange(xs.shape[1], dtype=jnp.int32)
    pa_k, pa_v, pb_k, pb_v, pb_ik, pm_k, pm_v = [], [], [], [], [], [], []
    sa_k, sa_v, sb_k, sb_v, sb_ik = [], [], [], [], []
    for l in range(DEPTH):
        qa, ka, va, za, qb, kb, vb, zb, qi, ki, wi, qm, zm, gates = _in_proj(xp, w_in[l], b_in[l])
        o_a = lax.map(lambda a: _moba_attend(a[0], a[1], a[2], pos_p, sl_a), (qa, ka, va))
        o_b = lax.map(lambda a: _dsa_attend(a[0], a[1], a[2], a[3], a[4], a[5], pos_p, sl_b), (qb, kb, vb, qi, wi, ki))
        mk, mv = _mem_kv(mem_prompt, w_mem_kv[l])
        o_m = _mem_attend(qm, mk, mv)
        xp = _merge(xp, o_a, za, o_b, zb, o_m, zm, gates, w_br_a[l], w_br_b[l], w_br_m[l], w_out[l], ln_g[l], ln_b[l])
        pa_k.append(ka); pa_v.append(va); pb_k.append(kb); pb_v.append(vb); pb_ik.append(ki)
        pm_k.append(mk); pm_v.append(mv)

        qa_s, ka_s, va_s, za_s, qb_s, kb_s, vb_s, zb_s, qi_s, ki_s, wi_s, qm_s, zm_s, gates_s = _in_proj(xs, w_in[l], b_in[l])

        def sample_seq(args, l=l):
            q_a, k_a, v_a, q_b, k_b, v_b, q_i, w_i, k_i, pt = args

            def past(cache):
                rows = cache[l, pt]
                return rows.reshape((rows.shape[0] * rows.shape[1],) + rows.shape[2:])

            ka_all = jnp.concatenate([past(cache_a_k), k_a], axis=0)
            va_all = jnp.concatenate([past(cache_a_v), v_a], axis=0)
            kb_all = jnp.concatenate([past(cache_b_k), k_b], axis=0)
            vb_all = jnp.concatenate([past(cache_b_v), v_b], axis=0)
            ki_all = jnp.concatenate([past(cache_b_idx_k), k_i], axis=0)
            out_a = _moba_attend(q_a, ka_all, va_all, pos_s, sl_a)
            out_b = _dsa_attend(q_b, kb_all, vb_all, q_i, w_i, ki_all, pos_s, sl_b)
            return out_a, out_b

        o_a_s, o_b_s = lax.map(sample_seq, (qa_s, ka_s, va_s, qb_s, kb_s, vb_s, qi_s, wi_s, ki_s, page_table))
        o_m_s = _mem_attend(qm_s, cache_mem_k[l], cache_mem_v[l])
        xs = _merge(xs, o_a_s, za_s, o_b_s, zb_s, o_m_s, zm_s, gates_s, w_br_a[l], w_br_b[l], w_br_m[l], w_out[l], ln_g[l], ln_b[l])
        sa_k.append(ka_s); sa_v.append(va_s); sb_k.append(kb_s); sb_v.append(vb_s); sb_ik.append(ki_s)
    return (xp, xs,
            jnp.stack(pa_k), jnp.stack(pa_v), jnp.stack(pb_k), jnp.stack(pb_v), jnp.stack(pb_ik),
            jnp.stack(pm_k), jnp.stack(pm_v),
            jnp.stack(sa_k), jnp.stack(sa_v), jnp.stack(sb_k), jnp.stack(sb_v), jnp.stack(sb_ik))
```

```python
import functools

import numpy as np
import jax
import jax.numpy as jnp
from jax import lax
from jax.experimental import pallas as pl
from jax.experimental.pallas import tpu as pltpu

F32 = jnp.float32
BF16 = jnp.bfloat16
I32 = jnp.int32

D_MODEL = 2048
HEAD_DIM = 128
H_A = 8
H_B = 8
H_M = 4
N_MEM = 256
MOBA_BLOCK = 256
MOBA_TOPK = 3
DSA_TOPK = 256
IDX_HEADS = 16
IDX_DIM = 64
PAGE_SIZE = 128
DEPTH = 1
W_A = H_A * HEAD_DIM
W_B = H_B * HEAD_DIM
W_M = H_M * HEAD_DIM
DEEPNORM_ALPHA = (2.0 * DEPTH) ** 0.25
LN_EPS = 1e-5
ATT_SCALE = HEAD_DIM ** -0.5
IDX_SCALE = IDX_DIM ** -0.5

LANES = 128
NEG_BIG = -1e30
INT_MIN = -2 ** 31
VMEM_LIMIT = 56 * 1024 * 1024

_NT = (((1,), (1,)), ((), ()))


def _params(*sem):
    return pltpu.CompilerParams(dimension_semantics=sem, vmem_limit_bytes=VMEM_LIMIT)


def _proj_kernel(x_ref, w_ref, b_ref, o_ref):
    x = x_ref[...].astype(BF16)
    o_ref[...] = jnp.dot(x, w_ref[...], preferred_element_type=F32) + b_ref[...]


def _proj(x, w, b, *, tm, tn, name):
    m, k = x.shape
    n = w.shape[1]
    assert m % tm == 0 and n % tn == 0
    return pl.pallas_call(
        _proj_kernel,
        grid=(m // tm, n // tn),
        in_specs=[pl.BlockSpec((tm, k), lambda i, j: (i, 0)),
                  pl.BlockSpec((k, tn), lambda i, j: (0, j)),
                  pl.BlockSpec((1, tn), lambda i, j: (0, j))],
        out_specs=pl.BlockSpec((tm, tn), lambda i, j: (i, j)),
        out_shape=jax.ShapeDtypeStruct((m, n), F32),
        compiler_params=_params("parallel", "arbitrary"),
        name=name,
    )(x, w, b)


def _softmax_pv(logits, v_bf16):
    m = jnp.max(logits, axis=1, keepdims=True)
    p = jnp.exp(logits - m)
    l = jnp.sum(p, axis=1, keepdims=True)
    o = jnp.dot(p.astype(BF16), v_bf16, preferred_element_type=F32)
    return o / l


def _topk_mask(score, n_sel, pos):
    score = jnp.where(score == 0.0, 0.0, score)
    bits = pltpu.bitcast(score, I32)
    key = jnp.where(bits < 0, bits ^ 0x7FFFFFFF, bits)
    k_f = float(n_sel)

    def count(pred):
        return jnp.sum(jnp.where(pred, 1.0, 0.0), axis=1, keepdims=True)

    t0 = jnp.where(count(key >= 0) >= k_f, 0, INT_MIN).astype(I32)

    def bit_step(b, t):
        cand = t | lax.shift_left(jnp.int32(1), 30 - b)
        return jnp.where(count(key >= cand) >= k_f, cand, t)

    thr = lax.fori_loop(0, 31, bit_step, t0)
    gt = key > thr
    eq = key == thr
    need = k_f - count(gt)
    n_bits = int(score.shape[1]).bit_length()

    def idx_step(b, p):
        cand = p | lax.shift_left(jnp.int32(1), n_bits - 1 - b)
        return jnp.where(count(eq & (pos < cand)) <= need, cand, p)

    p_lim = lax.fori_loop(0, n_bits, idx_step, jnp.zeros_like(thr))
    return gt | (eq & (pos < p_lim))


def _moba_prompt_kernel(slopes_ref, q_ref, k_ref, v_ref, e_ref, o_ref, kb_scr, vb_scr, kmean_scr, *, n_blk):
    h = pl.program_id(1)
    i = pl.program_id(2)
    seq = n_blk * MOBA_BLOCK

    @pl.when(i == 0)
    def _():
        k = k_ref[...]
        kb_scr[...] = k.astype(BF16)
        vb_scr[...] = v_ref[...].astype(BF16)
        kmean_scr[...] = jnp.zeros_like(kmean_scr)
        kmean_scr[0:n_blk, :] = jnp.sum(k.reshape(n_blk, MOBA_BLOCK, HEAD_DIM), axis=1) * (1.0 / MOBA_BLOCK)

    q = q_ref[...]
    gate = lax.dot_general(q, kmean_scr[...], _NT, precision=lax.Precision.HIGHEST,
                           preferred_element_type=F32)
    blk = lax.broadcasted_iota(I32, gate.shape, 1)
    valid = blk < i
    gate = jnp.where(valid, gate, -jnp.inf)
    rank = jnp.zeros(gate.shape, F32)
    for m in range(n_blk):
        gm = gate[:, m:m + 1]
        rank = rank + jnp.where((gm > gate) | ((gm == gate) & (blk > m)), 1.0, 0.0)
    sel = (valid & (rank < float(MOBA_TOPK))) | (blk == i)
    sel_keys = jnp.dot(jnp.where(sel, 1.0, 0.0).astype(BF16), e_ref[...],
                       preferred_element_type=F32)

    s = lax.dot_general(q.astype(BF16), kb_scr[...], _NT, preferred_element_type=F32)
    t = i * MOBA_BLOCK + lax.broadcasted_iota(I32, s.shape, 0)
    pos = lax.broadcasted_iota(I32, s.shape, 1)
    dist = (t - pos).astype(F32)
    logits = s * ATT_SCALE - slopes_ref[h] * dist
    logits = jnp.where((sel_keys > 0.5) & (dist >= 0.0), logits, -jnp.inf)
    o_ref[...] = _softmax_pv(logits, vb_scr[...])


def _moba_prompt(q, k, v, slopes, *, batch, seq):
    n_blk = seq // MOBA_BLOCK
    assert seq % MOBA_BLOCK == 0 and n_blk <= LANES
    expand = (np.arange(LANES)[:, None] == (np.arange(seq)[None, :] // MOBA_BLOCK))
    expand = jnp.asarray(expand, BF16)
    return pl.pallas_call(
        functools.partial(_moba_prompt_kernel, n_blk=n_blk),
        grid=(batch, H_A, n_blk),
        in_specs=[pl.BlockSpec(memory_space=pltpu.SMEM),
                  pl.BlockSpec((MOBA_BLOCK, HEAD_DIM), lambda b, h, i: (b * n_blk + i, h)),
                  pl.BlockSpec((seq, HEAD_DIM), lambda b, h, i: (b, h)),
                  pl.BlockSpec((seq, HEAD_DIM), lambda b, h, i: (b, h)),
                  pl.BlockSpec((LANES, seq), lambda b, h, i: (0, 0))],
        out_specs=pl.BlockSpec((MOBA_BLOCK, HEAD_DIM), lambda b, h, i: (b * n_blk + i, h)),
        out_shape=jax.ShapeDtypeStruct(q.shape, F32),
        scratch_shapes=[pltpu.VMEM((seq, HEAD_DIM), BF16), pltpu.VMEM((seq, HEAD_DIM), BF16),
                        pltpu.VMEM((LANES, HEAD_DIM), F32)],
        compiler_params=_params("parallel", "parallel", "arbitrary"),
        name="moba_prompt",
    )(slopes, q, k, v, expand)


def _indexer_scores(qi, wcol, ki_bf16):
    score = None
    for j in range(IDX_HEADS):
        qj = qi[:, j * IDX_DIM:(j + 1) * IDX_DIM].astype(BF16)
        rel = jnp.maximum(lax.dot_general(qj, ki_bf16, _NT, preferred_element_type=F32), 0.0)
        term = wcol(j) * rel
        score = term if score is None else score + term
    return score


def _dsa_select_prompt_kernel(qi_ref, kw_all_ref, kw_q_ref, mask_ref, *, tq, seq):
    i = pl.program_id(1)
    ki = kw_all_ref[:, 0:IDX_DIM].astype(BF16)
    w = kw_q_ref[:, IDX_DIM:IDX_DIM + IDX_HEADS] * (IDX_SCALE * IDX_HEADS ** -0.5)
    score = _indexer_scores(qi_ref[...], lambda j: w[:, j:j + 1], ki)
    t = i * tq + lax.broadcasted_iota(I32, score.shape, 0)
    pos = lax.broadcasted_iota(I32, score.shape, 1)
    ok = pos <= t
    score = jnp.where(ok, score, -jnp.inf)
    sel = _topk_mask(score, min(DSA_TOPK, seq // 4), pos) & ok
    mask_ref[...] = jnp.where(sel, 1.0, 0.0).astype(mask_ref.dtype)


def _dsa_select_prompt(qi, kw, *, batch, seq, tq):
    nq = seq // tq
    return pl.pallas_call(
        functools.partial(_dsa_select_prompt_kernel, tq=tq, seq=seq),
        grid=(batch, nq),
        in_specs=[pl.BlockSpec((tq, IDX_HEADS * IDX_DIM), lambda b, i: (b * nq + i, 0)),
                  pl.BlockSpec((seq, LANES), lambda b, i: (b, 0)),
                  pl.BlockSpec((tq, LANES), lambda b, i: (b * nq + i, 0))],
        out_specs=pl.BlockSpec((tq, seq), lambda b, i: (b * nq + i, 0)),
        out_shape=jax.ShapeDtypeStruct((batch * seq, seq), BF16),
        compiler_params=_params("parallel", "arbitrary"),
        name="dsa_select_prompt",
    )(qi, kw, kw)


def _dsa_attn_prompt_kernel(slopes_ref, q_ref, k_ref, v_ref, mask_ref, o_ref, kb_scr, vb_scr, *, tq):
    h = pl.program_id(1)
    i = pl.program_id(2)

    @pl.when(i == 0)
    def _():
        kb_scr[...] = k_ref[...].astype(BF16)
        vb_scr[...] = v_ref[...].astype(BF16)

    s = lax.dot_general(q_ref[...].astype(BF16), kb_scr[...], _NT, preferred_element_type=F32)
    t = i * tq + lax.broadcasted_iota(I32, s.shape, 0)
    pos = lax.broadcasted_iota(I32, s.shape, 1)
    dist = (t - pos).astype(F32)
    logits = s * ATT_SCALE - slopes_ref[H_A + h] * dist
    logits = jnp.where(mask_ref[...].astype(F32) > 0.5, logits, -jnp.inf)
    o_ref[...] = _softmax_pv(logits, vb_scr[...])


def _dsa_attn_prompt(q, k, v, mask, slopes, *, batch, seq, tq):
    nq = seq // tq
    return pl.pallas_call(
        functools.partial(_dsa_attn_prompt_kernel, tq=tq),
        grid=(batch, H_B, nq),
        in_specs=[pl.BlockSpec(memory_space=pltpu.SMEM),
                  pl.BlockSpec((tq, HEAD_DIM), lambda b, h, i: (b * nq + i, h)),
                  pl.BlockSpec((seq, HEAD_DIM), lambda b, h, i: (b, h)),
                  pl.BlockSpec((seq, HEAD_DIM), lambda b, h, i: (b, h)),
                  pl.BlockSpec((tq, seq), lambda b, h, i: (b * nq + i, 0))],
        out_specs=pl.BlockSpec((tq, HEAD_DIM), lambda b, h, i: (b * nq + i, h)),
        out_shape=jax.ShapeDtypeStruct(q.shape, F32),
        scratch_shapes=[pltpu.VMEM((seq, HEAD_DIM), BF16), pltpu.VMEM((seq, HEAD_DIM), BF16)],
        compiler_params=_params("parallel", "parallel", "arbitrary"),
        name="dsa_attn_prompt",
    )(slopes, q, k, v, mask)


def _mem_attn_kernel(q_ref, mk_ref, mv_ref, o_ref):
    for h in range(H_M):
        cols = slice(h * HEAD_DIM, (h + 1) * HEAD_DIM)
        s = lax.dot_general(q_ref[:, cols].astype(BF16), mk_ref[:, cols].astype(BF16), _NT,
                            preferred_element_type=F32) * ATT_SCALE
        o_ref[:, cols] = _softmax_pv(s, mv_ref[:, cols].astype(BF16))


def _mem_attn(qz, mk, mv, *, batch, rows, tq, name):
    nq = rows // tq
    return pl.pallas_call(
        _mem_attn_kernel,
        grid=(batch, nq),
        in_specs=[pl.BlockSpec((tq, W_M), lambda b, i: (b * nq + i, 0)),
                  pl.BlockSpec((N_MEM, W_M), lambda b, i: (b, 0)),
                  pl.BlockSpec((N_MEM, W_M), lambda b, i: (b, 0))],
        out_specs=pl.BlockSpec((tq, W_M), lambda b, i: (b * nq + i, 0)),
        out_shape=jax.ShapeDtypeStruct((batch * rows, W_M), F32),
        compiler_params=_params("parallel", "arbitrary"),
        name=name,
    )(qz, mk, mv)


def _silu(z):
    return z * (1.0 / (1.0 + jnp.exp(-z)))


def _sigmoid(z):
    return 1.0 / (1.0 + jnp.exp(-z))


def _merge_kernel(x_ref, oa_ref, za_ref, ob_ref, zb_ref, om_ref, zm_ref, g0_ref, g1_ref, g2_ref,
                  wa_ref, wb_ref, wm_ref, wo_ref, lng_ref, lnb_ref, y_ref):
    def branch(o_ref, z_ref, w_ref):
        a = (o_ref[...] * _silu(z_ref[...])).astype(BF16)
        return jnp.dot(a, w_ref[...], preferred_element_type=F32)

    mixed = _sigmoid(g0_ref[...]) * branch(oa_ref, za_ref, wa_ref)
    mixed = mixed + _sigmoid(g1_ref[...]) * branch(ob_ref, zb_ref, wb_ref)
    mixed = mixed + _sigmoid(g2_ref[...]) * branch(om_ref, zm_ref, wm_ref)
    y = jnp.dot(mixed.astype(BF16), wo_ref[...], preferred_element_type=F32)
    r = DEEPNORM_ALPHA * x_ref[...] + y
    mu = jnp.mean(r, axis=1, keepdims=True)
    c = r - mu
    var = jnp.mean(c * c, axis=1, keepdims=True)
    y_ref[...] = c * lax.rsqrt(var + LN_EPS) * lng_ref[...] + lnb_ref[...]


def _merge(x, oa, za, ob, zb, om, qz, gates, wa, wb, wm, wo, lng, lnb, *, tm, name):
    m = x.shape[0]
    row = lambda w: pl.BlockSpec((tm, w), lambda i: (i, 0))
    const = lambda a: pl.BlockSpec(a.shape, lambda i: (0, 0), pipeline_mode=pl.Buffered(1))
    return pl.pallas_call(
        _merge_kernel,
        grid=(m // tm,),
        in_specs=[row(D_MODEL), row(W_A), row(W_A), row(W_B), row(W_B), row(W_M),
                  pl.BlockSpec((tm, W_M), lambda i: (i, 1)),
                  pl.BlockSpec((tm, D_MODEL), lambda i: (i, 0)),
                  pl.BlockSpec((tm, D_MODEL), lambda i: (i, 1)),
                  pl.BlockSpec((tm, D_MODEL), lambda i: (i, 2)),
                  const(wa), const(wb), const(wm), const(wo), const(lng), const(lnb)],
        out_specs=row(D_MODEL),
        out_shape=jax.ShapeDtypeStruct((m, D_MODEL), F32),
        compiler_params=_params("parallel"),
        name=name,
    )(x, oa, za, ob, zb, om, qz, gates, gates, gates, wa, wb, wm, wo, lng, lnb)


def _stack_heads(a, n_heads):
    return jnp.concatenate([a[:, h * HEAD_DIM:(h + 1) * HEAD_DIM] for h in range(n_heads)], axis=0)


def _unstack_heads(o_ref, o, n_heads):
    r = o.shape[0] // n_heads
    for h in range(n_heads):
        o_ref[:, h * HEAD_DIM:(h + 1) * HEAD_DIM] = o[h * r:(h + 1) * r, :]


def _page_logits(q_stack_bf16, k_page_ref, slope_col, t_col, page_start, n_heads, n_q):
    s = lax.dot_general(q_stack_bf16, k_page_ref[...].astype(BF16), _NT, preferred_element_type=F32)
    row = lax.broadcasted_iota(I32, s.shape, 0)
    lane = lax.broadcasted_iota(I32, s.shape, 1)
    same_head = (lane % n_heads) == (row // n_q)
    pos = page_start + lane // n_heads
    dist = (t_col - pos).astype(F32)
    return s * ATT_SCALE - slope_col * dist, same_head


def _own_logits(q_stack_bf16, k_own_stack_bf16, slope_col, n_q):
    s = lax.dot_general(q_stack_bf16, k_own_stack_bf16, _NT, preferred_element_type=F32)
    row = lax.broadcasted_iota(I32, s.shape, 0)
    lane = lax.broadcasted_iota(I32, s.shape, 1)
    same_head = (lane // n_q) == (row // n_q)
    dist = ((row % n_q) - (lane % n_q)).astype(F32)
    return s * ATT_SCALE - slope_col * dist, same_head, dist


def _slope_col(slopes_ref, base, n_heads, n_q):
    row = lax.broadcasted_iota(I32, (n_heads * n_q, 1), 0)
    col = jnp.zeros((n_heads * n_q, 1), F32)
    for h in range(n_heads):
        col = jnp.where(row // n_q == h, slopes_ref[base + h], col)
    return col


def _moba_sample_kernel(pt_ref, slopes_ref, q_ref, kn_ref, vn_ref, kp_ref, vp_ref, o_ref,
                        m_scr, l_scr, acc_scr, ksum_scr, *, n_q, n_pages, past_len):
    p = pl.program_id(1)
    n_rows = H_A * n_q
    pages_per_blk = MOBA_BLOCK // PAGE_SIZE
    n_past_blk = n_pages // pages_per_blk
    qs = _stack_heads(q_ref[...], H_A)
    qs_bf = qs.astype(BF16)
    slope = _slope_col(slopes_ref, 0, H_A, n_q)
    t_col = past_len + lax.broadcasted_iota(I32, (n_rows, 1), 0) % n_q

    logits, same_head = _page_logits(qs_bf, kp_ref, slope, t_col, p * PAGE_SIZE, H_A, n_q)
    logits = jnp.where(same_head, logits, -jnp.inf)
    m = jnp.max(logits, axis=1, keepdims=True)
    e = jnp.exp(logits - m)
    m_scr[p] = m
    l_scr[p] = jnp.sum(e, axis=1, keepdims=True)
    acc_scr[p] = jnp.dot(e.astype(BF16), vp_ref[...].astype(BF16), preferred_element_type=F32)
    ksum_scr[p] = jnp.sum(kp_ref[...].reshape(PAGE_SIZE, H_A, HEAD_DIM), axis=0)

    @pl.when(p == n_pages - 1)
    def _():
        gates = []
        for n in range(n_past_blk):
            ks = ksum_scr[n * pages_per_blk]
            for r in range(1, pages_per_blk):
                ks = ks + ksum_scr[n * pages_per_blk + r]
            kmean = ks * (1.0 / MOBA_BLOCK)
            kmean_rows = jnp.concatenate(
                [jnp.broadcast_to(kmean[h:h + 1, :], (n_q, HEAD_DIM)) for h in range(H_A)], axis=0)
            gates.append(jnp.sum(qs * kmean_rows, axis=1, keepdims=True))
        keep = []
        for n in range(n_past_blk):
            rank = jnp.zeros((n_rows, 1), F32)
            for j in range(n_past_blk):
                if j != n:
                    beats = (gates[j] > gates[n]) | ((gates[j] == gates[n]) & (j < n))
                    rank = rank + jnp.where(beats, 1.0, 0.0)
            keep.append(rank < float(MOBA_TOPK))
        lo, same, dist = _own_logits(qs_bf, _stack_heads(kn_ref[...], H_A).astype(BF16), slope, n_q)
        lo = jnp.where(same & (dist >= 0.0), lo, -jnp.inf)
        m_tot = jnp.max(lo, axis=1, keepdims=True)
        for pg in range(n_pages):
            m_tot = jnp.where(keep[pg // pages_per_blk], jnp.maximum(m_tot, m_scr[pg]), m_tot)
        e_own = jnp.exp(lo - m_tot)
        l_tot = jnp.sum(e_own, axis=1, keepdims=True)
        acc = jnp.dot(e_own.astype(BF16), _stack_heads(vn_ref[...], H_A).astype(BF16),
                      preferred_element_type=F32)
        for pg in range(n_pages):
            w = jnp.where(keep[pg // pages_per_blk], jnp.exp(m_scr[pg] - m_tot), 0.0)
            l_tot = l_tot + w * l_scr[pg]
            acc = acc + w * acc_scr[pg]
        _unstack_heads(o_ref, acc / l_tot, H_A)


def _moba_sample(q, k_new, v_new, k_pages, v_pages, page_table, slopes, *, n_seq, n_q, n_pages):
    assert MOBA_BLOCK % PAGE_SIZE == 0 and (n_pages * PAGE_SIZE) % MOBA_BLOCK == 0
    assert n_q <= MOBA_BLOCK
    n_rows = H_A * n_q
    rows = pl.BlockSpec((n_q, W_A), lambda b, p, pt: (b, 0))
    page = pl.BlockSpec((None, PAGE_SIZE * H_A, HEAD_DIM), lambda b, p, pt: (pt[b * n_pages + p], 0, 0))
    return pl.pallas_call(
        functools.partial(_moba_sample_kernel, n_q=n_q, n_pages=n_pages, past_len=n_pages * PAGE_SIZE),
        grid_spec=pltpu.PrefetchScalarGridSpec(
            num_scalar_prefetch=1,
            grid=(n_seq, n_pages),
            in_specs=[pl.BlockSpec(memory_space=pltpu.SMEM), rows, rows, rows, page, page],
            out_specs=rows,
            scratch_shapes=[pltpu.VMEM((n_pages, n_rows, 1), F32), pltpu.VMEM((n_pages, n_rows, 1), F32),
                            pltpu.VMEM((n_pages, n_rows, HEAD_DIM), F32),
                            pltpu.VMEM((n_pages, H_A, HEAD_DIM), F32)]),
        out_shape=jax.ShapeDtypeStruct(q.shape, F32),
        compiler_params=_params("parallel", "arbitrary"),
        name="moba_sample",
    )(page_table, slopes, q, k_new, v_new, k_pages, v_pages)


def _dsa_select_sample_kernel(pt_ref, qi_ref, kw_ref, kip_ref, mask_ref, qs_scr, w_scr, sc_scr,
                              *, n_q, n_pages, past_len):
    p = pl.program_id(1)

    @pl.when(p == 0)
    def _():
        qi = qi_ref[...]
        qs_scr[...] = jnp.concatenate(
            [qi[:, j * IDX_DIM:(j + 1) * IDX_DIM] for j in range(IDX_HEADS)], axis=0)
        w = kw_ref[:, IDX_DIM:IDX_DIM + IDX_HEADS] * (IDX_SCALE * IDX_HEADS ** -0.5)
        w_scr[...] = jnp.concatenate(
            [jnp.broadcast_to(w[:, j:j + 1], (n_q, LANES)) for j in range(IDX_HEADS)], axis=0)

    def scores(ki_bf16):
        rel = jnp.maximum(lax.dot_general(qs_scr[...].astype(BF16), ki_bf16, _NT,
                                          preferred_element_type=F32), 0.0)
        wr = w_scr[:, 0:rel.shape[1]] * rel
        acc = wr[0:n_q, :]
        for j in range(1, IDX_HEADS):
            acc = acc + wr[j * n_q:(j + 1) * n_q, :]
        return acc

    sc_scr[p] = scores(kip_ref[...].astype(BF16))

    @pl.when(p == n_pages - 1)
    def _():
        own = scores(jnp.concatenate(
            [kw_ref[:, 0:IDX_DIM], jnp.zeros((LANES - n_q, IDX_DIM), F32)], axis=0).astype(BF16))
        qrow = lax.broadcasted_iota(I32, (n_q, LANES), 0)
        lane = lax.broadcasted_iota(I32, (n_q, LANES), 1)
        sc_scr[n_pages] = jnp.where(lane <= qrow, own, -jnp.inf)
        score = jnp.concatenate([sc_scr[pg] for pg in range(n_pages + 1)], axis=1)
        pos = lax.broadcasted_iota(I32, score.shape, 1)
        n_keys = past_len + n_q
        sel = _topk_mask(score, min(DSA_TOPK, n_keys // 4), pos) & (score > -jnp.inf)
        self_f = jnp.where(sel, 1.0, 0.0)
        for pg in range(n_pages + 1):
            mask_ref[pg] = self_f[:, pg * LANES:(pg + 1) * LANES]


def _dsa_select_sample(qi, kw, ki_pages, page_table, *, n_seq, n_q, n_pages):
    assert n_q <= LANES
    rows = lambda w: pl.BlockSpec((n_q, w), lambda b, p, pt: (b, 0))
    return pl.pallas_call(
        functools.partial(_dsa_select_sample_kernel, n_q=n_q, n_pages=n_pages, past_len=n_pages * PAGE_SIZE),
        grid_spec=pltpu.PrefetchScalarGridSpec(
            num_scalar_prefetch=1,
            grid=(n_seq, n_pages),
            in_specs=[rows(IDX_HEADS * IDX_DIM), rows(LANES),
                      pl.BlockSpec((None, PAGE_SIZE, IDX_DIM), lambda b, p, pt: (pt[b * n_pages + p], 0, 0))],
            out_specs=pl.BlockSpec((None, n_pages + 1, n_q, LANES), lambda b, p, pt: (b, 0, 0, 0)),
            scratch_shapes=[pltpu.VMEM((IDX_HEADS * n_q, IDX_DIM), F32),
                            pltpu.VMEM((IDX_HEADS * n_q, LANES), F32),
                            pltpu.VMEM((n_pages + 1, n_q, LANES), F32)]),
        out_shape=jax.ShapeDtypeStruct((n_seq, n_pages + 1, n_q, LANES), F32),
        compiler_params=_params("parallel", "arbitrary"),
        name="dsa_select_sample",
    )(page_table, qi, kw, ki_pages)


def _dsa_attn_sample_kernel(pt_ref, slopes_ref, q_ref, kn_ref, vn_ref, mask_ref, ep_ref, eo_ref,
                            kp_ref, vp_ref, o_ref, m_scr, l_scr, acc_scr, *, n_q, n_pages, past_len):
    p = pl.program_id(1)
    n_rows = H_B * n_q
    qs_bf = _stack_heads(q_ref[...], H_B).astype(BF16)
    slope = _slope_col(slopes_ref, H_A, H_B, n_q)
    t_col = past_len + lax.broadcasted_iota(I32, (n_rows, 1), 0) % n_q

    def update(logits, v_bf16):
        m_old = m_scr[...]
        m_new = jnp.maximum(m_old, jnp.max(logits, axis=1, keepdims=True))
        alpha = jnp.exp(m_old - m_new)
        e = jnp.exp(logits - m_new)
        m_scr[...] = m_new
        l_scr[...] = alpha * l_scr[...] + jnp.sum(e, axis=1, keepdims=True)
        acc_scr[...] = alpha * acc_scr[...] + jnp.dot(e.astype(BF16), v_bf16, preferred_element_type=F32)

    def expand(mask_q, e_ref):
        wide = jnp.dot(mask_q.astype(BF16), e_ref[...], preferred_element_type=F32)
        return jnp.concatenate([wide] * H_B, axis=0)

    @pl.when(p == 0)
    def _():
        m_scr[...] = jnp.full(m_scr.shape, NEG_BIG, F32)
        l_scr[...] = jnp.zeros_like(l_scr)
        acc_scr[...] = jnp.zeros_like(acc_scr)
        lo, same, _ = _own_logits(qs_bf, _stack_heads(kn_ref[...], H_B).astype(BF16), slope, n_q)
        picked = expand(mask_ref[n_pages], eo_ref) > 0.5
        update(jnp.where(same & picked, lo, -jnp.inf), _stack_heads(vn_ref[...], H_B).astype(BF16))

    logits, same_head = _page_logits(qs_bf, kp_ref, slope, t_col, p * PAGE_SIZE, H_B, n_q)
    picked = expand(mask_ref[p], ep_ref) > 0.5
    update(jnp.where(same_head & picked, logits, -jnp.inf), vp_ref[...].astype(BF16))

    @pl.when(p == n_pages - 1)
    def _():
        _unstack_heads(o_ref, acc_scr[...] / l_scr[...], H_B)


def _dsa_attn_sample(q, k_new, v_new, mask, k_pages, v_pages, page_table, slopes, *, n_seq, n_q, n_pages):
    n_rows = H_B * n_q
    lane = np.arange(LANES)[:, None]
    e_page = jnp.asarray(lane == (np.arange(PAGE_SIZE * H_B)[None, :] // H_B), BF16)
    e_own = jnp.asarray(lane == (np.arange(n_rows)[None, :] % n_q), BF16)
    rows = pl.BlockSpec((n_q, W_B), lambda b, p, pt: (b, 0))
    page = pl.BlockSpec((None, PAGE_SIZE * H_B, HEAD_DIM), lambda b, p, pt: (pt[b * n_pages + p], 0, 0))
    const = lambda a: pl.BlockSpec(a.shape, lambda b, p, pt: (0, 0))
    return pl.pallas_call(
        functools.partial(_dsa_attn_sample_kernel, n_q=n_q, n_pages=n_pages, past_len=n_pages * PAGE_SIZE),
        grid_spec=pltpu.PrefetchScalarGridSpec(
            num_scalar_prefetch=1,
            grid=(n_seq, n_pages),
            in_specs=[pl.BlockSpec(memory_space=pltpu.SMEM), rows, rows, rows,
                      pl.BlockSpec((None, n_pages + 1, n_q, LANES), lambda b, p, pt: (b, 0, 0, 0)),
                      const(e_page), const(e_own), page, page],
            out_specs=rows,
            scratch_shapes=[pltpu.VMEM((n_rows, 1), F32), pltpu.VMEM((n_rows, 1), F32),
                            pltpu.VMEM((n_rows, HEAD_DIM), F32)]),
        out_shape=jax.ShapeDtypeStruct(q.shape, F32),
        compiler_params=_params("parallel", "arbitrary"),
        name="dsa_attn_sample",
    )(page_table, slopes, q, k_new, v_new, mask, e_page, e_own, k_pages, v_pages)


def _alibi_slopes():
    n = H_A + H_B
    return 2.0 ** (-8.0 * jnp.arange(1, n + 1, dtype=F32) / n)


def _in_proj(x2d, w_in, b_in, tag):
    c_qa, c_ki, c_qm, c_g, c_end = 0, 9 * 1024, 9 * 1024 + IDX_DIM + IDX_HEADS, 0, 0
    c_g = c_qm + 2 * W_M
    c_end = c_g + 3 * D_MODEL
    assert w_in.shape[1] == c_end
    tm = 512
    wb = w_in.astype(BF16)
    b2 = b_in.reshape(1, -1)
    out = {}
    names = ("qa", "ka", "va", "za", "qb", "kb", "vb", "zb", "qi")
    for n, name in enumerate(names):
        cols = slice(c_qa + n * 1024, c_qa + (n + 1) * 1024)
        out[name] = _proj(x2d, wb[:, cols], b2[:, cols], tm=tm, tn=1024, name=f"proj_{name}_{tag}")
    pad = LANES - (c_qm - c_ki)
    out["kw"] = _proj(x2d, jnp.pad(wb[:, c_ki:c_qm], ((0, 0), (0, pad))),
                      jnp.pad(b2[:, c_ki:c_qm], ((0, 0), (0, pad))), tm=tm, tn=LANES, name=f"proj_kw_{tag}")
    out["qz"] = _proj(x2d, wb[:, c_qm:c_g], b2[:, c_qm:c_g], tm=tm, tn=1024, name=f"proj_qz_{tag}")
    out["gates"] = _proj(x2d, wb[:, c_g:], b2[:, c_g:], tm=tm, tn=1024, name=f"proj_gates_{tag}")
    return out


def kernel(x_prompt, x_sample, cache_a_k, cache_a_v, cache_b_k, cache_b_v, cache_b_idx_k,
           cache_mem_k, cache_mem_v, page_table, mem_prompt,
           w_in, b_in, w_mem_kv, w_br_a, w_br_b, w_br_m, w_out, ln_g, ln_b):
    assert w_in.shape[0] == DEPTH == 1
    batch, seq, _ = x_prompt.shape
    n_seq, n_q, _ = x_sample.shape
    n_pages = page_table.shape[1]
    n_phys = cache_a_k.shape[1]
    slopes = _alibi_slopes()
    l = 0
    w_a, w_b, w_m, w_o = (w[l].astype(BF16) for w in (w_br_a, w_br_b, w_br_m, w_out))
    lng, lnb = ln_g[l].reshape(1, -1), ln_b[l].reshape(1, -1)

    xp = x_prompt.reshape(batch * seq, D_MODEL)
    hp = _in_proj(xp, w_in[l], b_in[l], "p")
    o_a = _moba_prompt(hp["qa"], hp["ka"], hp["va"], slopes, batch=batch, seq=seq)
    mask = _dsa_select_prompt(hp["qi"], hp["kw"], batch=batch, seq=seq, tq=128)
    o_b = _dsa_attn_prompt(hp["qb"], hp["kb"], hp["vb"], mask, slopes, batch=batch, seq=seq, tq=128)
    w_kv = w_mem_kv[l].astype(BF16)
    zero_b = jnp.zeros((1, W_M), F32)
    mem2d = mem_prompt.reshape(batch * N_MEM, D_MODEL)
    mk = _proj(mem2d, w_kv[:, :W_M], zero_b, tm=512, tn=W_M, name="proj_mem_k")
    mv = _proj(mem2d, w_kv[:, W_M:], zero_b, tm=512, tn=W_M, name="proj_mem_v")
    o_m = _mem_attn(hp["qz"], mk, mv, batch=batch, rows=seq, tq=512, name="mem_attn_prompt")
    yp = _merge(xp, o_a, hp["za"], o_b, hp["zb"], o_m, hp["qz"], hp["gates"],
                w_a, w_b, w_m, w_o, lng, lnb, tm=128, name="merge_prompt")

    xs = x_sample.reshape(n_seq * n_q, D_MODEL)
    hs = _in_proj(xs, w_in[l], b_in[l], "s")
    pt = page_table.reshape(-1)
    pages = lambda c, nh: c[l].reshape(n_phys, PAGE_SIZE * nh, HEAD_DIM)
    o_a_s = _moba_sample(hs["qa"], hs["ka"], hs["va"], pages(cache_a_k, H_A), pages(cache_a_v, H_A), pt, slopes,
                         n_seq=n_seq, n_q=n_q, n_pages=n_pages)
    mask_s = _dsa_select_sample(hs["qi"], hs["kw"], cache_b_idx_k[l], pt, n_seq=n_seq, n_q=n_q, n_pages=n_pages)
    o_b_s = _dsa_attn_sample(hs["qb"], hs["kb"], hs["vb"], mask_s, pages(cache_b_k, H_B), pages(cache_b_v, H_B),
                             pt, slopes, n_seq=n_seq, n_q=n_q, n_pages=n_pages)
    o_m_s = _mem_attn(hs["qz"], cache_mem_k[l].reshape(n_seq * N_MEM, W_M), cache_mem_v[l].reshape(n_seq * N_MEM, W_M),
                      batch=n_seq, rows=n_q, tq=n_q, name="mem_attn_sample")
    ys = _merge(xs, o_a_s, hs["za"], o_b_s, hs["zb"], o_m_s, hs["qz"], hs["gates"],
                w_a, w_b, w_m, w_o, lng, lnb, tm=128, name="merge_sample")

    heads = lambda a, n, t, nh: a.reshape(DEPTH, n, t, nh, HEAD_DIM)
    return (yp.reshape(batch, seq, D_MODEL), ys.reshape(n_seq, n_q, D_MODEL),
            heads(hp["ka"], batch, seq, H_A), heads(hp["va"], batch, seq, H_A),
            heads(hp["kb"], batch, seq, H_B), heads(hp["vb"], batch, seq, H_B),
            hp["kw"][:, :IDX_DIM].reshape(DEPTH, batch, seq, IDX_DIM),
            heads(mk, batch, N_MEM, H_M), heads(mv, batch, N_MEM, H_M),
            heads(hs["ka"], n_seq, n_q, H_A), heads(hs["va"], n_seq, n_q, H_A),
            heads(hs["kb"], n_seq, n_q, H_B), heads(hs["vb"], n_seq, n_q, H_B),
            hs["kw"][:, :IDX_DIM].reshape(DEPTH, n_seq, n_q, IDX_DIM))
```
